```python
import math
import jax, jax.numpy as jnp
from jax import lax
import numpy as np

D_MODEL = 1024
BATCH = 16
SEQ = 2048
DEPTH = 2

N_GROUPS = 4
GROUP_WIDTH = D_MODEL // N_GROUPS
MIX_WIDTH = N_GROUPS * GROUP_WIDTH
HEAD_DIM = 64
GROUP_HEADS = GROUP_WIDTH // HEAD_DIM
N_IN_SLICES = 10
ROPE_THETA = 10000.0
MOBA_BLOCK = 256
MOBA_TOPK = 3
MOBA_QCHUNK = 16
DILATED_CONFIGS = ((128, 1), (512, 4), (2048, 16))
GMLP_CHUNK = 128
GMLP_GROUPS = 4
CONV_WIDTH = 31
D_FF_DENSE = 2816
N_EXPERTS = 8
TOP_K = 2
D_FF_EXPERT = 3584
MOE_ROW_BLOCK = 256
N_DENSE_LAYERS = (DEPTH + 1) // 2
N_MOE_LAYERS = DEPTH // 2
NORM_EPS = 1e-6
NEG_INF = -1e30
ATTN_SCALE = HEAD_DIM ** -0.5

kernel_name = 'hybrid_moba_dilated_gmlp_conformer_moe_block'


def rmsnorm(x, g):
    xf = x.astype(jnp.float32)
    y = xf * lax.rsqrt(jnp.mean(xf * xf, axis=-1, keepdims=True) + NORM_EPS)
    return (y * g.astype(jnp.float32)).astype(x.dtype)


def layernorm(x):
    xf = x.astype(jnp.float32)
    mu = jnp.mean(xf, axis=-1, keepdims=True)
    var = jnp.mean(jnp.square(xf - mu), axis=-1, keepdims=True)
    return ((xf - mu) * lax.rsqrt(var + NORM_EPS)).astype(x.dtype)


def rope(x, positions):
    half = HEAD_DIM // 2
    inv_freq = jnp.power(ROPE_THETA, -jnp.arange(half, dtype=jnp.float32) / half)
    ang = positions.astype(jnp.float32)[..., None] * inv_freq
    cos = jnp.cos(ang)[:, :, None, :]
    sin = jnp.sin(ang)[:, :, None, :]
    x1 = x[..., :half].astype(jnp.float32)
    x2 = x[..., half:].astype(jnp.float32)
    return jnp.concatenate([x1 * cos - x2 * sin, x2 * cos + x1 * sin], axis=-1).astype(x.dtype)


def swiglu(x, w_gate, w_up, w_down):
    return (jax.nn.silu(x @ w_gate) * (x @ w_up)) @ w_down


def moba_attention(q, k, v):
    B, S, H, D = q.shape
    s_pad = -(-S // MOBA_BLOCK) * MOBA_BLOCK
    n_blk = s_pad // MOBA_BLOCK
    pad = ((0, 0), (0, s_pad - S), (0, 0), (0, 0))
    q, k, v = (jnp.pad(t, pad).transpose(0, 2, 1, 3) for t in (q, k, v))
    kb = k.reshape(B, H, n_blk, MOBA_BLOCK, D)
    vb = v.reshape(B, H, n_blk, MOBA_BLOCK, D)
    k_mean = jnp.mean(kb.astype(jnp.float32), axis=3)
    gate = jnp.einsum('bhsd,bhnd->bhsn', q.astype(jnp.float32), k_mean)
    q_blk = jnp.arange(s_pad) // MOBA_BLOCK
    past = jnp.arange(n_blk)[None, :] < q_blk[:, None]
    gate = jnp.where(past, gate, NEG_INF)
    n_sel = min(MOBA_TOPK, n_blk)
    _, sel = lax.top_k(gate, n_sel)
    sel_ok = sel < q_blk[:, None]
    n_chunks = s_pad // MOBA_QCHUNK

    def to_chunks(t):
        return jnp.moveaxis(t.reshape(B, H, n_chunks, MOBA_QCHUNK, *t.shape[3:]), 2, 0)

    b_idx = jnp.arange(B)[:, None, None, None]
    h_idx = jnp.arange(H)[None, :, None, None]
    key_off = jnp.arange(MOBA_BLOCK)
    q_off = jnp.arange(MOBA_QCHUNK)

    def chunk(args):
        qc, selc, okc, start = args
        blk = start // MOBA_BLOCK
        k_own = lax.dynamic_index_in_dim(kb, blk, axis=2, keepdims=False)
        v_own = lax.dynamic_index_in_dim(vb, blk, axis=2, keepdims=False)
        causal = (blk * MOBA_BLOCK + key_off)[None, :] <= (start + q_off)[:, None]
        s_own = jnp.einsum('bhqd,bhkd->bhqk', qc, k_own).astype(jnp.float32) * ATTN_SCALE
        s_own = jnp.where(causal, s_own, NEG_INF)
        k_g = kb[b_idx, h_idx, selc]
        v_g = vb[b_idx, h_idx, selc]
        s_sel = jnp.einsum('bhqd,bhqnkd->bhqnk', qc, k_g).astype(jnp.float32) * ATTN_SCALE
        s_sel = jnp.where(okc[..., None], s_sel, NEG_INF).reshape(B, H, MOBA_QCHUNK, n_sel * MOBA_BLOCK)
        p = jax.nn.softmax(jnp.concatenate([s_own, s_sel], axis=-1), axis=-1)
        p_own = p[..., :MOBA_BLOCK]
        p_sel = p[..., MOBA_BLOCK:].reshape(B, H, MOBA_QCHUNK, n_sel, MOBA_BLOCK)
        return (jnp.einsum('bhqk,bhkd->bhqd', p_own, v_own)
                + jnp.einsum('bhqnk,bhqnkd->bhqd', p_sel, v_g))

    starts = jnp.arange(n_chunks, dtype=jnp.int32) * MOBA_QCHUNK
    out = lax.map(chunk, (to_chunks(q), to_chunks(sel), to_chunks(sel_ok), starts))
    out = jnp.moveaxis(out, 0, 2).reshape(B, H, s_pad, D)[:, :, :S]
    return out.transpose(0, 2, 1, 3).astype(q.dtype)


def strided_window_attention(q, k, v, span, dil):
    B, S, H, D = q.shape
    L = S // dil
    blk = span
    l_pad = -(-L // blk) * blk
    nb = l_pad // blk

    def sub(t):
        t = t.reshape(B, L, dil, H, D).transpose(0, 2, 3, 1, 4)
        t = jnp.pad(t, ((0, 0), (0, 0), (0, 0), (0, l_pad - L), (0, 0)))
        return t.reshape(B, dil, H, nb, blk, D)

    def band_keys(t):
        prev = jnp.pad(t, ((0, 0), (0, 0), (0, 0), (1, 0), (0, 0), (0, 0)))[:, :, :, :-1]
        return jnp.concatenate([prev, t], axis=4)

    qs = sub(q)
    k_span = band_keys(sub(k))
    v_span = band_keys(sub(v))
    qi = jnp.arange(blk)[:, None]
    kj = jnp.arange(2 * blk)[None, :]
    rel = blk + qi - kj
    band = (rel >= 0) & (rel <= span)
    not_first = jnp.arange(nb)[:, None, None] > 0
    mask = band[None] & (not_first | (kj >= blk)[None])
    s = jnp.einsum('bghnqd,bghnkd->bghnqk', qs, k_span).astype(jnp.float32) * ATTN_SCALE
    s = jnp.where(mask, s, NEG_INF)
    m = jnp.max(s, axis=-1, keepdims=True)
    p = jnp.exp(s - m)
    den = jnp.sum(p, axis=-1, keepdims=True)
    o = jnp.einsum('bghnqk,bghnkd->bghnqd', p, v_span) / den
    lse = (m + jnp.log(den))[..., 0]
    o = o.reshape(B, dil, H, l_pad, D)[:, :, :, :L].transpose(0, 3, 1, 2, 4).reshape(B, S, H, D)
    lse = lse.reshape(B, dil, H, l_pad)[..., :L].transpose(0, 3, 1, 2).reshape(B, S, H)
    return o, lse


def dilated_attention(q, k, v):
    outs, lses = zip(*[strided_window_attention(q, k, v, w // d, d) for (w, d) in DILATED_CONFIGS])
    o = jnp.stack(outs)
    wts = jax.nn.softmax(jnp.stack(lses), axis=0)
    return jnp.sum(wts[..., None] * o, axis=0).astype(q.dtype)


def spatial_gating(u, v, ws, b):
    B, S, C = v.shape
    n_chunk = S // GMLP_CHUNK
    vg = layernorm(v).reshape(B, n_chunk, GMLP_CHUNK, GMLP_GROUPS, C // GMLP_GROUPS)
    causal = jnp.tril(jnp.ones((GMLP_CHUNK, GMLP_CHUNK), dtype=bool))
    w = jnp.where(causal, ws, 0)
    z = jnp.einsum('gts,bnsgc->bntgc', w, vg) + b.T[:, :, None]
    return u * z.reshape(B, S, C)


def conv_module(a, g, conv_w, conv_b, ln_g, ln_b):
    x = a * jax.nn.sigmoid(g)
    C = x.shape[-1]
    y = lax.conv_general_dilated(x, conv_w[:, None, :], window_strides=(1,),
                                 padding=((CONV_WIDTH - 1, 0),),
                                 dimension_numbers=('NWC', 'WIO', 'NWC'),
                                 feature_group_count=C) + conv_b
    y = layernorm(y) * ln_g + ln_b
    return jax.nn.silu(y)


def hybrid_mixer(h, positions, w_in, w_out, group_g, gmlp_ws, gmlp_b,
                 conv_w, conv_b, conv_ln_g, conv_ln_b):
    B, S, _ = h.shape
    qa, ka, va, qb, kb, vb, u, vg, ga, gg = jnp.split(h @ w_in, N_IN_SLICES, axis=-1)

    def heads(t):
        return t.reshape(B, S, GROUP_HEADS, HEAD_DIM)

    o_a = moba_attention(rope(heads(qa), positions), rope(heads(ka), positions), heads(va))
    o_b = dilated_attention(rope(heads(qb), positions), rope(heads(kb), positions), heads(vb))
    o_c = spatial_gating(jax.nn.gelu(u), jax.nn.gelu(vg), gmlp_ws, gmlp_b)
    o_d = conv_module(ga, gg, conv_w, conv_b, conv_ln_g, conv_ln_b)
    mix = jnp.concatenate([o_a.reshape(B, S, GROUP_WIDTH), o_b.reshape(B, S, GROUP_WIDTH), o_c, o_d], axis=-1)
    mix = rmsnorm(mix.reshape(B, S, N_GROUPS, GROUP_WIDTH), group_g).reshape(B, S, MIX_WIDTH)
    return mix @ w_out


def moe_swiglu(h, router_w, router_b, w_gate, w_up, w_down):
    B, S, D = h.shape
    n_tok = B * S
    n_assign = n_tok * TOP_K
    xt = h.reshape(n_tok, D)
    logits = (xt @ router_w).astype(jnp.float32) + router_b.astype(jnp.float32)
    top_logit, top_idx = lax.top_k(logits, TOP_K)
    gates = jax.nn.softmax(top_logit, axis=-1).reshape(n_assign)
    expert = top_idx.reshape(n_assign)
    order = jnp.argsort(expert)
    e_sorted = expert[order]
    tok_sorted = order // TOP_K
    g_sorted = gates[order]
    counts = jnp.bincount(expert, length=N_EXPERTS)
    starts = jnp.cumsum(counts) - counts
    padded = (counts + MOE_ROW_BLOCK - 1) // MOE_ROW_BLOCK * MOE_ROW_BLOCK
    pad_end = jnp.cumsum(padded)
    pad_start = pad_end - padded
    dest = pad_start[e_sorted] + jnp.arange(n_assign) - starts[e_sorted]
    n_blocks = -(-n_assign // MOE_ROW_BLOCK) + N_EXPERTS
    rows = jnp.zeros((n_blocks * MOE_ROW_BLOCK, D), h.dtype).at[dest].set(xt[tok_sorted])
    block_expert = jnp.clip(jnp.searchsorted(pad_end, jnp.arange(n_blocks) * MOE_ROW_BLOCK, side='right'),
                            0, N_EXPERTS - 1)

    def run_block(args):
        xb, e = args
        return swiglu(xb, w_gate[e], w_up[e], w_down[e])

    out = lax.map(run_block, (rows.reshape(n_blocks, MOE_ROW_BLOCK, D), block_expert)).reshape(-1, D)
    y = jax.ops.segment_sum(out[dest] * g_sorted[:, None], tok_sorted, num_segments=n_tok)
    return y.reshape(B, S, D).astype(h.dtype)


def setup_inputs(seed: int = 0) -> dict:
    key = jax.random.key(seed)
    ks = jax.random.split(key, 26)

    def nrm(k, shape, scale):
        return jax.random.normal(k, shape, jnp.float32) * scale

    def gain(k, shape):
        return 1.0 + 0.05 * jax.random.normal(k, shape, jnp.float32)

    positions = (jax.random.randint(ks[2], (BATCH, 1), 0, 4096, dtype=jnp.int32)
                 + jnp.arange(SEQ, dtype=jnp.int32)[None, :])
    return {
        'x': nrm(ks[0], (BATCH, SEQ, D_MODEL), 1.0),
        'c': nrm(ks[1], (BATCH, D_MODEL), 1.0),
        'positions': positions,
        'ada_w': nrm(ks[3], (DEPTH, D_MODEL, 6 * D_MODEL), 0.5 * D_MODEL ** -0.5),
        'ada_b': nrm(ks[4], (DEPTH, 6 * D_MODEL), 0.02),
        'mix_pre_g': gain(ks[5], (DEPTH, D_MODEL)),
        'mix_post_g': gain(ks[6], (DEPTH, D_MODEL)),
        'ffn_pre_g': gain(ks[7], (DEPTH, D_MODEL)),
        'ffn_post_g': gain(ks[8], (DEPTH, D_MODEL)),
        'w_in': nrm(ks[9], (DEPTH, D_MODEL, N_IN_SLICES * GROUP_WIDTH), D_MODEL ** -0.5),
        'w_out': nrm(ks[10], (DEPTH, MIX_WIDTH, D_MODEL), MIX_WIDTH ** -0.5),
        'group_out_g': gain(ks[11], (DEPTH, N_GROUPS, GROUP_WIDTH)),
        'gmlp_ws': nrm(ks[12], (DEPTH, GMLP_GROUPS, GMLP_CHUNK, GMLP_CHUNK), GMLP_CHUNK ** -0.5),
        'gmlp_b': gain(ks[13], (DEPTH, GMLP_GROUPS, GMLP_CHUNK)),
        'conv_w': nrm(ks[14], (DEPTH, CONV_WIDTH, GROUP_WIDTH), CONV_WIDTH ** -0.5),
        'conv_b': nrm(ks[15], (DEPTH, GROUP_WIDTH), 0.02),
        'conv_ln_g': gain(ks[16], (DEPTH, GROUP_WIDTH)),
        'conv_ln_b': nrm(ks[17], (DEPTH, GROUP_WIDTH), 0.02),
        'ffn_w_gate': nrm(ks[18], (N_DENSE_LAYERS, D_MODEL, D_FF_DENSE), D_MODEL ** -0.5),
        'ffn_w_up': nrm(ks[19], (N_DENSE_LAYERS, D_MODEL, D_FF_DENSE), D_MODEL ** -0.5),
        'ffn_w_down': nrm(ks[20], (N_DENSE_LAYERS, D_FF_DENSE, D_MODEL), D_FF_DENSE ** -0.5),
        'router_w': nrm(ks[21], (N_MOE_LAYERS, D_MODEL, N_EXPERTS), D_MODEL ** -0.5),
        'router_b': nrm(ks[22], (N_MOE_LAYERS, N_EXPERTS), 0.01),
        'moe_w_gate': nrm(ks[23], (N_MOE_LAYERS, N_EXPERTS, D_MODEL, D_FF_EXPERT), D_MODEL ** -0.5),
        'moe_w_up': nrm(ks[24], (N_MOE_LAYERS, N_EXPERTS, D_MODEL, D_FF_EXPERT), D_MODEL ** -0.5),
        'moe_w_down': nrm(ks[25], (N_MOE_LAYERS, N_EXPERTS, D_FF_EXPERT, D_MODEL), D_FF_EXPERT ** -0.5),
    }


def reference(x, c, positions, ada_w, ada_b, mix_pre_g, mix_post_g, ffn_pre_g, ffn_post_g,
              w_in, w_out, group_out_g, gmlp_ws, gmlp_b, conv_w, conv_b, conv_ln_g, conv_ln_b,
              ffn_w_gate, ffn_w_up, ffn_w_down, router_w, router_b, moe_w_gate, moe_w_up, moe_w_down):
    c_act = jax.nn.silu(c)
    for layer in range(DEPTH):
        mod = c_act @ ada_w[layer] + ada_b[layer]
        sh_m, sc_m, gt_m, sh_f, sc_f, gt_f = (t[:, None, :] for t in jnp.split(mod, 6, axis=-1))
        h = rmsnorm(x, mix_pre_g[layer]) * (1 + sc_m) + sh_m
        y = hybrid_mixer(h, positions, w_in[layer], w_out[layer], group_out_g[layer],
                         gmlp_ws[layer], gmlp_b[layer], conv_w[layer], conv_b[layer],
                         conv_ln_g[layer], conv_ln_b[layer])
        x = x + gt_m * rmsnorm(y, mix_post_g[layer])
        h = rmsnorm(x, ffn_pre_g[layer]) * (1 + sc_f) + sh_f
        i = layer // 2
        if layer % 2 == 0:
            y = swiglu(h, ffn_w_gate[i], ffn_w_up[i], ffn_w_down[i])
        else:
            y = moe_swiglu(h, router_w[i], router_b[i], moe_w_gate[i], moe_w_up[i], moe_w_down[i])
        x = x + gt_f * rmsnorm(y, ffn_post_g[layer])
    return x
```

```python
import functools

import jax
import jax.numpy as jnp
from jax import lax
from jax.experimental import pallas as pl
from jax.experimental.pallas import tpu as pltpu

F32 = jnp.float32
BF16 = jnp.bfloat16

LANES = 128
GROUP_WIDTH = 256
HEAD_DIM = 64
N_IN_SLICES = 10
ROPE_THETA = 10000.0
MOBA_BLOCK = 256
MOBA_TOPK = 3
DILATED_CONFIGS = ((128, 1), (512, 4), (2048, 16))
DIL_SPAN = 128
GMLP_CHUNK = 128
GMLP_GROUPS = 4
CONV_WIDTH = 31
N_EXPERTS = 8
TOP_K = 2
NORM_EPS = 1e-6
NEG_INF = -1e30
MASK_BIAS = -30000.0
ATTN_SCALE = HEAD_DIM ** -0.5
VMEM_LIMIT = 56 * 1024 * 1024

ROW_TILE = 512
MOE_ROW_TILE = 512
MOE_FF_TILE = 512
GATHER_CHUNK = 512


def _params(*sem):
    return pltpu.CompilerParams(dimension_semantics=sem, vmem_limit_bytes=VMEM_LIMIT)


def _silu(x):
    return x * jax.nn.sigmoid(x)


def _gelu_tanh(x):
    return 0.5 * x * (1.0 + jnp.tanh(0.7978845608028654 * (x + 0.044715 * (x * x * x))))


def _rms(x):
    return x * lax.rsqrt(jnp.mean(x * x, axis=-1, keepdims=True) + NORM_EPS)


def _dot(a, b):
    return jnp.dot(a, b, preferred_element_type=F32)


def _dot_nt(a, b):
    return lax.dot_general(a, b, (((1,), (1,)), ((), ())), preferred_element_type=F32)


def _ada_kernel(c_ref, w_ref, b_ref, o_ref):
    ca = _silu(c_ref[...]).astype(BF16)
    o_ref[0] = _dot(ca, w_ref[0].astype(BF16)) + b_ref[0]


def _ada_mod(c, ada_w, ada_b):
    depth, d, six_d = ada_w.shape
    b = c.shape[0]
    tn = 1024
    mod = pl.pallas_call(
        _ada_kernel,
        grid=(depth, six_d // tn),
        in_specs=[pl.BlockSpec((b, d), lambda l, j: (0, 0)),
                  pl.BlockSpec((1, d, tn), lambda l, j: (l, 0, j)),
                  pl.BlockSpec((1, 1, tn), lambda l, j: (l, 0, j))],
        out_specs=pl.BlockSpec((1, b, tn), lambda l, j: (l, 0, j)),
        out_shape=jax.ShapeDtypeStruct((depth, b, six_d), F32),
        compiler_params=_params("arbitrary", "arbitrary"),
        name="ada_mod",
    )(c, ada_w, ada_b.reshape(depth, 1, six_d))
    return mod.reshape(depth, b, 6, d)


def _rope_table_kernel(pos_ref, freq_ref, sign_ref, cos_ref, sin_ref):
    ang = pos_ref[0].astype(F32) * freq_ref[...]
    cos_ref[0] = jnp.cos(ang)
    sin_ref[0] = jnp.sin(ang) * sign_ref[...]


def _rope_tables(positions):
    b, s = positions.shape
    half = HEAD_DIM // 2
    inv_freq = jnp.power(ROPE_THETA, -jnp.arange(half, dtype=F32) / half)
    lane = jnp.arange(LANES)
    freq = inv_freq[lane % half].reshape(1, LANES)
    sign = jnp.where((lane % HEAD_DIM) < half, -1.0, 1.0).astype(F32).reshape(1, LANES)
    tm = ROW_TILE
    spec = pl.BlockSpec((1, tm, LANES), lambda i, j: (i, j, 0))
    vec = pl.BlockSpec((1, LANES), lambda i, j: (0, 0))
    return pl.pallas_call(
        _rope_table_kernel,
        grid=(b, s // tm),
        in_specs=[pl.BlockSpec((1, tm, 1), lambda i, j: (i, j, 0)), vec, vec],
        out_specs=[spec, spec],
        out_shape=[jax.ShapeDtypeStruct((b, s, LANES), F32)] * 2,
        compiler_params=_params("arbitrary", "arbitrary"),
        name="rope_tables",
    )(positions.reshape(b, s, 1), freq, sign)


ROPE_SLICES = (0, 1, 3, 4)
GELU_SLICES = (6, 7)


def _inproj_kernel(x_ref, mod_ref, g_ref, w_ref, cos_ref, sin_ref, o_ref, h_ref):
    m = mod_ref[0]
    h = _rms(x_ref[0]) * g_ref[...] * (1.0 + m[1:2]) + m[0:1]
    h_ref[...] = h.astype(BF16)
    cos = cos_ref[0]
    sin = sin_ref[0]
    lane = lax.broadcasted_iota(jnp.int32, cos.shape, 1)
    first_half = (lane % HEAD_DIM) < (HEAD_DIM // 2)

    def rope(t):
        swapped = jnp.where(first_half, pltpu.roll(t, LANES - HEAD_DIM // 2, 1),
                            pltpu.roll(t, HEAD_DIM // 2, 1))
        return t * cos + swapped * sin

    for j in range(N_IN_SLICES):
        lo = j * GROUP_WIDTH
        blk = _dot(h_ref[...], w_ref[:, lo:lo + GROUP_WIDTH])
        if j in ROPE_SLICES:
            blk = jnp.concatenate([rope(blk[:, :LANES]), rope(blk[:, LANES:])], axis=1)
        elif j in GELU_SLICES:
            blk = _gelu_tanh(blk)
        o_ref[0, :, lo:lo + GROUP_WIDTH] = blk.astype(BF16)


def _in_proj(x, mod, pre_g, w_in, cos, sin):
    b, s, d = x.shape
    n_out = w_in.shape[1]
    tm = ROW_TILE
    return pl.pallas_call(
        _inproj_kernel,
        grid=(b, s // tm),
        in_specs=[pl.BlockSpec((1, tm, d), lambda i, j: (i, j, 0)),
                  pl.BlockSpec((1, 6, d), lambda i, j: (i, 0, 0)),
                  pl.BlockSpec((1, d), lambda i, j: (0, 0)),
                  pl.BlockSpec((d, n_out), lambda i, j: (0, 0)),
                  pl.BlockSpec((1, tm, LANES), lambda i, j: (i, j, 0)),
                  pl.BlockSpec((1, tm, LANES), lambda i, j: (i, j, 0))],
        out_specs=pl.BlockSpec((1, tm, n_out), lambda i, j: (i, j, 0)),
        out_shape=jax.ShapeDtypeStruct((b, s, n_out), BF16),
        scratch_shapes=[pltpu.VMEM((tm, d), BF16)],
        compiler_params=_params("arbitrary", "arbitrary"),
        name="in_proj",
    )(x, mod, pre_g.reshape(1, d), w_in, cos, sin)


def _moba_kernel(q_ref, k_ref, v_ref, o_ref):
    s_len = q_ref.shape[1]
    nb = s_len // MOBA_BLOCK
    qb = q_ref[0]
    q = qb.astype(F32)
    k = k_ref[0].astype(F32)
    v = v_ref[0]
    lane = lax.broadcasted_iota(jnp.int32, (s_len, LANES), 1)
    row = lax.broadcasted_iota(jnp.int32, (s_len, LANES), 0)
    row_blk = row // MOBA_BLOCK
    kmean = jnp.mean(k.reshape(nb, MOBA_BLOCK, LANES), axis=1)
    km_lane = lax.broadcasted_iota(jnp.int32, (nb, LANES), 1)
    zeros_a = jnp.zeros((HEAD_DIM, LANES), F32)
    zeros_b = jnp.zeros((HEAD_DIM - nb, LANES), F32)
    outs = []
    for h in range(2):
        head_lanes = (lane // HEAD_DIM) == h
        off = HEAD_DIM * (1 - h)
        km_h = jnp.where((km_lane // HEAD_DIM) == h, kmean, 0.0)
        pieces = [km_h, zeros_b, zeros_a] if off == 0 else [zeros_a, km_h, zeros_b]
        km_mat = jnp.concatenate(pieces, axis=0)
        km_hi = km_mat.astype(BF16)
        km_lo = (km_mat - km_hi.astype(F32)).astype(BF16)
        gate = _dot_nt(qb, km_hi) + _dot_nt(qb, km_lo)
        col = lane - off
        past = (col >= 0) & (col < row_blk)
        g = jnp.where(past, gate, NEG_INF)
        sel = jnp.zeros((s_len, LANES), jnp.bool_)
        for _ in range(MOBA_TOPK):
            mx = jnp.max(g, axis=1, keepdims=True)
            first = jnp.min(jnp.where(g == mx, lane, 2 * LANES), axis=1, keepdims=True)
            pick = lane == first
            sel = sel | pick
            g = jnp.where(pick, -3e38, g)
        allowed = (sel & past) | (col == row_blk)
        bias = jnp.where(allowed | (col < 0) | (col >= nb), 0.0, MASK_BIAS)
        q_aug = jnp.where(head_lanes, q * ATTN_SCALE, bias).astype(BF16)
        onehot = jnp.where(col == row_blk, 1.0, 0.0)
        k_aug = jnp.where(head_lanes, k, onehot).astype(BF16)
        o_blocks = []
        for n in range(nb):
            n_keys = (n + 1) * MOBA_BLOCK
            sc = _dot_nt(q_aug[n * MOBA_BLOCK:n_keys], k_aug[:n_keys])
            ci = lax.broadcasted_iota(jnp.int32, sc.shape, 1)
            ri = lax.broadcasted_iota(jnp.int32, sc.shape, 0)
            sc = jnp.where(ci <= ri + n * MOBA_BLOCK, sc, NEG_INF)
            mx = jnp.max(sc, axis=1, keepdims=True)
            p = jnp.exp(sc - mx)
            den = jnp.sum(p, axis=1, keepdims=True)
            o_blocks.append(_dot(p.astype(BF16), v[:n_keys]) / den)
        outs.append(jnp.concatenate(o_blocks, axis=0))
    o_ref[0] = jnp.where((lane // HEAD_DIM) == 0, outs[0], outs[1]).astype(BF16)


def _head_pair_spec(s, col_block):
    return pl.BlockSpec((1, s, LANES), lambda i, p: (i, 0, col_block + p))


def _moba(proj):
    b, s, _ = proj.shape
    assert s // MOBA_BLOCK == 8, "kernel lays the block gates out in 8 lanes"
    pairs = GROUP_WIDTH // LANES
    return pl.pallas_call(
        _moba_kernel,
        grid=(b, pairs),
        in_specs=[_head_pair_spec(s, 0 * pairs), _head_pair_spec(s, 1 * pairs), _head_pair_spec(s, 2 * pairs)],
        out_specs=pl.BlockSpec((1, s, LANES), lambda i, p: (i, 0, p)),
        out_shape=jax.ShapeDtypeStruct((b, s, GROUP_WIDTH), BF16),
        compiler_params=_params("arbitrary", "arbitrary"),
        name="moba",
    )(proj, proj, proj)


DIL_PAD = DIL_SPAN * max(d for _, d in DILATED_CONFIGS)


def _dilated_kernel(q_ref, k_ref, v_ref, o_ref, qs, ks, vs, m_s, l_s, acc_s):
    s_len = q_ref.shape[1]
    qs[...] = q_ref[0].astype(F32) * ATTN_SCALE
    ks[0:DIL_PAD] = jnp.zeros((DIL_PAD, LANES), F32)
    vs[0:DIL_PAD] = jnp.zeros((DIL_PAD, LANES), F32)
    ks[DIL_PAD:] = k_ref[0].astype(F32)
    vs[DIL_PAD:] = v_ref[0].astype(F32)
    m_s[...] = jnp.full((s_len, LANES), NEG_INF, F32)
    l_s[...] = jnp.zeros((s_len, LANES), F32)
    acc_s[...] = jnp.zeros((s_len, LANES), F32)
    blk = DIL_SPAN
    lane = lax.broadcasted_iota(jnp.int32, (blk, LANES), 1)
    head0 = lane < HEAD_DIM
    qi = lax.broadcasted_iota(jnp.int32, (blk, 2 * blk), 0)
    kj = lax.broadcasted_iota(jnp.int32, (blk, 2 * blk), 1)
    band = (kj >= qi) & (kj <= qi + DIL_SPAN)
    for window, dil in DILATED_CONFIGS:
        assert window // dil == DIL_SPAN and (s_len // dil) % blk == 0

        def unit(u, carry, dil=dil):
            r = u % dil
            i = u // dil
            base = r + i * (blk * dil)
            if dil == 1:
                q_rows = pl.ds(base, blk)
                k_rows = pl.ds(DIL_PAD + base - blk, 2 * blk)
            else:
                q_rows = pl.ds(base, blk, stride=dil)
                k_rows = pl.ds(DIL_PAD + base - blk * dil, 2 * blk, stride=dil)
            qv = qs[q_rows, :]
            kb = ks[k_rows, :].astype(BF16)
            vb = vs[k_rows, :].astype(BF16)
            mask = band & ((kj >= blk) | (i > 0))
            m_old = m_s[q_rows, :]
            l_old = l_s[q_rows, :]
            acc_old = acc_s[q_rows, :]
            m_new, alpha, l_new, pv = [], [], [], []
            for h in range(2):
                qh = jnp.where(head0 if h == 0 else ~head0, qv, 0.0).astype(BF16)
                sc = jnp.where(mask, _dot_nt(qh, kb), NEG_INF)
                mo = m_old[:, h * HEAD_DIM:h * HEAD_DIM + 1]
                lo = l_old[:, h * HEAD_DIM:h * HEAD_DIM + 1]
                mn = jnp.maximum(mo, jnp.max(sc, axis=1, keepdims=True))
                a = jnp.exp(mo - mn)
                p = jnp.exp(sc - mn)
                m_new.append(mn)
                alpha.append(a)
                l_new.append(a * lo + jnp.sum(p, axis=1, keepdims=True))
                pv.append(_dot(p.astype(BF16), vb))

            def both(t):
                return jnp.where(head0, t[0], t[1])

            m_s[q_rows, :] = both(m_new)
            l_s[q_rows, :] = both(l_new)
            acc_s[q_rows, :] = both(alpha) * acc_old + both(pv)
            return carry

        lax.fori_loop(0, s_len // blk, unit, 0)
    o_ref[0] = (acc_s[...] / l_s[...]).astype(BF16)


def _dilated(proj):
    b, s, _ = proj.shape
    pairs = GROUP_WIDTH // LANES
    return pl.pallas_call(
        _dilated_kernel,
        grid=(b, pairs),
        in_specs=[_head_pair_spec(s, 3 * pairs), _head_pair_spec(s, 4 * pairs), _head_pair_spec(s, 5 * pairs)],
        out_specs=pl.BlockSpec((1, s, LANES), lambda i, p: (i, 0, p)),
        out_shape=jax.ShapeDtypeStruct((b, s, GROUP_WIDTH), BF16),
        scratch_shapes=[pltpu.VMEM((s, LANES), F32),
                        pltpu.VMEM((DIL_PAD + s, LANES), F32),
                        pltpu.VMEM((DIL_PAD + s, LANES), F32),
                        pltpu.VMEM((s, LANES), F32),
                        pltpu.VMEM((s, LANES), F32),
                        pltpu.VMEM((s, LANES), F32)],
        compiler_params=_params("arbitrary", "arbitrary"),
        name="dilated",
    )(proj, proj, proj)


def _gmlp_kernel(u_ref, v_ref, ws_ref, bias_ref, o_ref):
    tm = u_ref.shape[1]
    t = GMLP_CHUNK
    gw = GROUP_WIDTH // GMLP_GROUPS
    ti = lax.broadcasted_iota(jnp.int32, (t, t), 0)
    si = lax.broadcasted_iota(jnp.int32, (t, t), 1)
    lane = lax.broadcasted_iota(jnp.int32, (t, GROUP_WIDTH), 1)
    w = [jnp.where(si <= ti, ws_ref[g], 0.0).astype(BF16) for g in range(GMLP_GROUPS)]
    for c in range(tm // t):
        v = v_ref[0, c * t:(c + 1) * t, :].astype(F32)
        mu = jnp.mean(v, axis=-1, keepdims=True)
        vc = v - mu
        vn = vc * lax.rsqrt(jnp.mean(vc * vc, axis=-1, keepdims=True) + NORM_EPS)
        z = bias_ref[...]
        for g in range(GMLP_GROUPS):
            z = z + _dot(w[g], jnp.where((lane // gw) == g, vn, 0.0).astype(BF16))
        o_ref[0, c * t:(c + 1) * t, :] = (u_ref[0, c * t:(c + 1) * t, :].astype(F32) * z).astype(BF16)


def _gmlp(proj, ws, bias):
    b, s, _ = proj.shape
    tm = ROW_TILE
    gw = GROUP_WIDTH // GMLP_GROUPS
    bias_full = jnp.repeat(bias.T, gw, axis=1)
    return pl.pallas_call(
        _gmlp_kernel,
        grid=(b, s // tm),
        in_specs=[pl.BlockSpec((1, tm, GROUP_WIDTH), lambda i, j: (i, j, 6)),
                  pl.BlockSpec((1, tm, GROUP_WIDTH), lambda i, j: (i, j, 7)),
                  pl.BlockSpec(ws.shape, lambda i, j: (0, 0, 0)),
                  pl.BlockSpec(bias_full.shape, lambda i, j: (0, 0))],
        out_specs=pl.BlockSpec((1, tm, GROUP_WIDTH), lambda i, j: (i, j, 0)),
        out_shape=jax.ShapeDtypeStruct((b, s, GROUP_WIDTH), BF16),
        compiler_params=_params("arbitrary", "arbitrary"),
        name="gmlp",
    )(proj, proj, ws, bias_full)


CONV_PAD = 32


def _conv_kernel(a_ref, g_ref, w_ref, b_ref, lng_ref, lnb_ref, o_ref, xp):
    s_len = a_ref.shape[1]
    xp[0:CONV_PAD] = jnp.zeros((CONV_PAD, GROUP_WIDTH), F32)
    xp[CONV_PAD:] = a_ref[0].astype(F32) * jax.nn.sigmoid(g_ref[0].astype(F32))
    tr = 128
    shift = CONV_PAD - (CONV_WIDTH - 1)

    def tile(i, carry):
        t0 = pl.multiple_of(i * tr, tr)
        acc = jnp.zeros((tr, GROUP_WIDTH), F32) + b_ref[...]
        win = xp[pl.ds(t0, tr + CONV_PAD), :]
        for j in range(CONV_WIDTH):
            acc = acc + win[shift + j:shift + j + tr] * w_ref[j:j + 1, :]
        mu = jnp.mean(acc, axis=-1, keepdims=True)
        yc = acc - mu
        y = yc * lax.rsqrt(jnp.mean(yc * yc, axis=-1, keepdims=True) + NORM_EPS)
        y = y * lng_ref[...] + lnb_ref[...]
        o_ref[0, pl.ds(t0, tr), :] = _silu(y).astype(BF16)
        return carry

    lax.fori_loop(0, s_len // tr, tile, 0)


def _conv(proj, conv_w, conv_b, ln_g, ln_b):
    b, s, _ = proj.shape
    vec = pl.BlockSpec((1, GROUP_WIDTH), lambda i: (0, 0))
    return pl.pallas_call(
        _conv_kernel,
        grid=(b,),
        in_specs=[pl.BlockSpec((1, s, GROUP_WIDTH), lambda i: (i, 0, 8)),
                  pl.BlockSpec((1, s, GROUP_WIDTH), lambda i: (i, 0, 9)),
                  pl.BlockSpec((CONV_WIDTH, GROUP_WIDTH), lambda i: (0, 0)),
                  vec, vec, vec],
        out_specs=pl.BlockSpec((1, s, GROUP_WIDTH), lambda i: (i, 0, 0)),
        out_shape=jax.ShapeDtypeStruct((b, s, GROUP_WIDTH), BF16),
        scratch_shapes=[pltpu.VMEM((CONV_PAD + s, GROUP_WIDTH), F32)],
        compiler_params=_params("arbitrary"),
        name="conv",
    )(proj, proj, conv_w, conv_b.reshape(1, -1), ln_g.reshape(1, -1), ln_b.reshape(1, -1))


def _outproj_kernel(oa_ref, ob_ref, oc_ref, od_ref, x_ref, mod_ref, gg_ref, w_ref, pg_ref, o_ref):
    y = None
    for g, ref in enumerate((oa_ref, ob_ref, oc_ref, od_ref)):
        t = _rms(ref[0].astype(F32)) * gg_ref[g:g + 1, :]
        part = _dot(t.astype(BF16), w_ref[g * GROUP_WIDTH:(g + 1) * GROUP_WIDTH, :])
        y = part if y is None else y + part
    gate = mod_ref[0][2:3]
    o_ref[0] = x_ref[0] + gate * (_rms(y) * pg_ref[...])


def _out_proj(groups, x, mod, group_g, w_out, post_g):
    b, s, d = x.shape
    tm = ROW_TILE
    gspec = pl.BlockSpec((1, tm, GROUP_WIDTH), lambda i, j: (i, j, 0))
    return pl.pallas_call(
        _outproj_kernel,
        grid=(b, s // tm),
        in_specs=[gspec, gspec, gspec, gspec,
                  pl.BlockSpec((1, tm, d), lambda i, j: (i, j, 0)),
                  pl.BlockSpec((1, 6, d), lambda i, j: (i, 0, 0)),
                  pl.BlockSpec(group_g.shape, lambda i, j: (0, 0)),
                  pl.BlockSpec(w_out.shape, lambda i, j: (0, 0)),
                  pl.BlockSpec((1, d), lambda i, j: (0, 0))],
        out_specs=pl.BlockSpec((1, tm, d), lambda i, j: (i, j, 0)),
        out_shape=jax.ShapeDtypeStruct((b, s, d), F32),
        compiler_params=_params("arbitrary", "arbitrary"),
        name="out_proj",
    )(*groups, x, mod, group_g, w_out, post_g.reshape(1, d))


FF_CHUNK = 256


def _ffn_kernel(x_ref, mod_ref, g_ref, wg_ref, wu_ref, wd_ref, pg_ref, o_ref, h_ref, a_ref):
    m = mod_ref[0]
    h_ref[...] = (_rms(x_ref[0]) * g_ref[...] * (1.0 + m[4:5]) + m[3:4]).astype(BF16)
    d_ff = wg_ref.shape[1]
    for c in range(d_ff // FF_CHUNK):
        lo = c * FF_CHUNK
        gate = _dot(h_ref[...], wg_ref[:, lo:lo + FF_CHUNK])
        up = _dot(h_ref[...], wu_ref[:, lo:lo + FF_CHUNK])
        a_ref[:, lo:lo + FF_CHUNK] = (_silu(gate) * up).astype(BF16)
    y = _dot(a_ref[...], wd_ref[...])
    o_ref[0] = x_ref[0] + m[5:6] * (_rms(y) * pg_ref[...])


def _dense_ffn(x, mod, pre_g, w_gate, w_up, w_down, post_g):
    b, s, d = x.shape
    d_ff = w_gate.shape[1]
    tm = ROW_TILE
    resident = functools.partial(pl.BlockSpec, pipeline_mode=pl.Buffered(1))
    return pl.pallas_call(
        _ffn_kernel,
        grid=(b, s // tm),
        in_specs=[pl.BlockSpec((1, tm, d), lambda i, j: (i, j, 0)),
                  pl.BlockSpec((1, 6, d), lambda i, j: (i, 0, 0)),
                  pl.BlockSpec((1, d), lambda i, j: (0, 0)),
                  resident((d, d_ff), lambda i, j: (0, 0)),
                  resident((d, d_ff), lambda i, j: (0, 0)),
                  resident((d_ff, d), lambda i, j: (0, 0)),
                  pl.BlockSpec((1, d), lambda i, j: (0, 0))],
        out_specs=pl.BlockSpec((1, tm, d), lambda i, j: (i, j, 0)),
        out_shape=jax.ShapeDtypeStruct((b, s, d), F32),
        scratch_shapes=[pltpu.VMEM((tm, d), BF16), pltpu.VMEM((tm, d_ff), BF16)],
        compiler_params=_params("arbitrary", "arbitrary"),
        name="dense_ffn",
    )(x, mod, pre_g.reshape(1, d), w_gate, w_up, w_down, post_g.reshape(1, d))


def _split_bf16(t):
    hi = t.astype(BF16)
    return hi, (t - hi.astype(F32)).astype(BF16)


def _router_kernel(x_ref, mod_ref, g_ref, rw_ref, rb_ref, h_ref, gate_ref, idx_ref):
    m = mod_ref[0]
    h = _rms(x_ref[0]) * g_ref[...] * (1.0 + m[4:5]) + m[3:4]
    h_hi, h_lo = _split_bf16(h)
    h_ref[0] = h_hi
    w_hi, w_lo = _split_bf16(rw_ref[...])
    logits = _dot(h_hi, w_hi) + _dot(h_lo, w_hi) + _dot(h_hi, w_lo) + rb_ref[...]
    lane = lax.broadcasted_iota(jnp.int32, logits.shape, 1)
    l1 = jnp.max(logits, axis=1, keepdims=True)
    i1 = jnp.min(jnp.where(logits == l1, lane, 2 * LANES), axis=1, keepdims=True)
    rest = jnp.where(lane == i1, -3e38, logits)
    l2 = jnp.max(rest, axis=1, keepdims=True)
    i2 = jnp.min(jnp.where(rest == l2, lane, 2 * LANES), axis=1, keepdims=True)
    e2 = jnp.exp(l2 - l1)
    den = 1.0 + e2
    gate_ref[0] = jnp.where(lane == 0, 1.0 / den, jnp.where(lane == 1, e2 / den, 0.0))
    idx_ref[0] = jnp.where(lane == 0, i1, jnp.where(lane == 1, i2, 0))


def _router(x, mod, pre_g, router_w, router_b):
    b, s, d = x.shape
    tm = ROW_TILE
    rw = jnp.pad(router_w, ((0, 0), (0, LANES - N_EXPERTS)))
    rb = jnp.pad(router_b, (0, LANES - N_EXPERTS), constant_values=NEG_INF).reshape(1, LANES)
    lane_spec = pl.BlockSpec((1, tm, LANES), lambda i, j: (i, j, 0))
    return pl.pallas_call(
        _router_kernel,
        grid=(b, s // tm),
        in_specs=[pl.BlockSpec((1, tm, d), lambda i, j: (i, j, 0)),
                  pl.BlockSpec((1, 6, d), lambda i, j: (i, 0, 0)),
                  pl.BlockSpec((1, d), lambda i, j: (0, 0)),
                  pl.BlockSpec((d, LANES), lambda i, j: (0, 0)),
                  pl.BlockSpec((1, LANES), lambda i, j: (0, 0))],
        out_specs=[pl.BlockSpec((1, tm, d), lambda i, j: (i, j, 0)), lane_spec, lane_spec],
        out_shape=[jax.ShapeDtypeStruct((b, s, d), BF16),
                   jax.ShapeDtypeStruct((b, s, LANES), F32),
                   jax.ShapeDtypeStruct((b, s, LANES), jnp.int32)],
        compiler_params=_params("arbitrary", "arbitrary"),
        name="router",
    )(x, mod, pre_g.reshape(1, d), rw, rb)


def _row_copy(src_ref, dst_ref, sem, src_row, dst_row):
    return pltpu.make_async_copy(src_ref.at[src_row], dst_ref.at[dst_row], sem)


def _dispatch_kernel(dest_ref, h_ref, rows_in_ref, rows_ref, sem):
    del rows_in_ref
    base = pl.program_id(0) * GATHER_CHUNK

    def issue(a, carry):
        _row_copy(h_ref, rows_ref, sem, (base + a) // TOP_K, dest_ref[0, 0, a]).start()
        return carry

    lax.fori_loop(0, GATHER_CHUNK, issue, 0)

    def drain(a, carry):
        _row_copy(h_ref, rows_ref, sem, 0, 0).wait()
        return carry

    lax.fori_loop(0, GATHER_CHUNK, drain, 0)


def _dispatch(h3, dest, n_rows):
    n_assign = dest.shape[0]
    sub = h3.shape[1]
    rows0 = jnp.zeros((n_rows, sub, LANES), h3.dtype)
    return pl.pallas_call(
        _dispatch_kernel,
        grid=(n_assign // GATHER_CHUNK,),
        in_specs=[pl.BlockSpec((1, 1, GATHER_CHUNK), lambda i: (i, 0, 0), memory_space=pltpu.SMEM),
                  pl.BlockSpec(memory_space=pl.ANY),
                  pl.BlockSpec(memory_space=pl.ANY)],
        out_specs=pl.BlockSpec(memory_space=pl.ANY),
        out_shape=jax.ShapeDtypeStruct(rows0.shape, rows0.dtype),
        scratch_shapes=[pltpu.SemaphoreType.DMA(())],
        input_output_aliases={2: 0},
        compiler_params=pltpu.CompilerParams(dimension_semantics=("arbitrary",), has_side_effects=True),
        name="moe_dispatch",
    )(dest.reshape(n_assign // GATHER_CHUNK, 1, GATHER_CHUNK), h3, rows0)


def _expert_kernel(te_ref, nv_ref, x_ref, gr_ref, wg_ref, wu_ref, wd_ref, o_ref, acc_ref):
    i = pl.program_id(0)
    j = pl.program_id(1)
    last = pl.num_programs(1) - 1

    @pl.when(i < nv_ref[0])
    def _():
        x = x_ref[...]
        a = (_silu(_dot(x, wg_ref[0])) * _dot(x, wu_ref[0])).astype(BF16)
        part = _dot(a, wd_ref[0])

        @pl.when(j == 0)
        def _():
            acc_ref[...] = part

        @pl.when(j > 0)
        def _():
            acc_ref[...] += part

        @pl.when(j == last)
        def _():
            o_ref[...] = (acc_ref[...] * gr_ref[...]).astype(o_ref.dtype)

    @pl.when((i >= nv_ref[0]) & (j == last))
    def _():
        o_ref[...] = jnp.zeros(o_ref.shape, o_ref.dtype)


def _experts(rows, gate_rows, tile_expert, n_valid, w_gate, w_up, w_down):
    n_rows, d = rows.shape
    d_ff = w_gate.shape[2]
    tm, tf = MOE_ROW_TILE, MOE_FF_TILE
    n_ff = d_ff // tf

    def ff_idx(i, j, nv):
        return jnp.where(i < nv[0], j, n_ff - 1)

    return pl.pallas_call(
        _expert_kernel,
        grid_spec=pltpu.PrefetchScalarGridSpec(
            num_scalar_prefetch=2,
            grid=(n_rows // tm, n_ff),
            in_specs=[pl.BlockSpec((tm, d), lambda i, j, te, nv: (i, 0)),
                      pl.BlockSpec((tm, 1), lambda i, j, te, nv: (i, 0)),
                      pl.BlockSpec((1, d, tf), lambda i, j, te, nv: (te[i], 0, ff_idx(i, j, nv))),
                      pl.BlockSpec((1, d, tf), lambda i, j, te, nv: (te[i], 0, ff_idx(i, j, nv))),
                      pl.BlockSpec((1, tf, d), lambda i, j, te, nv: (te[i], ff_idx(i, j, nv), 0))],
            out_specs=pl.BlockSpec((tm, d), lambda i, j, te, nv: (i, 0)),
            scratch_shapes=[pltpu.VMEM((tm, d), F32)]),
        out_shape=jax.ShapeDtypeStruct((n_rows, d), BF16),
        compiler_params=_params("arbitrary", "arbitrary"),
        name="moe_experts",
    )(tile_expert, n_valid, rows, gate_rows, w_gate, w_up, w_down)


COMBINE_TOKENS = 256


def _combine_kernel(dest_ref, y_ref, x_ref, gt_ref, pg_ref, o_ref, buf, sem):
    tc = COMBINE_TOKENS

    def issue(a, carry):
        _row_copy(y_ref, buf.at[a % TOP_K], sem, dest_ref[0, 0, a], a // TOP_K).start()
        return carry

    lax.fori_loop(0, tc * TOP_K, issue, 0)

    def drain(a, carry):
        _row_copy(y_ref, buf.at[0], sem, 0, 0).wait()
        return carry

    lax.fori_loop(0, tc * TOP_K, drain, 0)
    y = buf[0].astype(F32)
    for kk in range(1, TOP_K):
        y = y + buf[kk].astype(F32)
    ms = jnp.sum(jnp.sum(y * y, axis=2, keepdims=True), axis=1, keepdims=True) / (y.shape[1] * y.shape[2])
    o_ref[...] = x_ref[...] + gt_ref[0] * (y * lax.rsqrt(ms + NORM_EPS) * pg_ref[...])


def _combine(y3, dest, x3, gt3, pg3, seq_len):
    n_tok, sub, _ = x3.shape
    tc = COMBINE_TOKENS
    tiles_per_seq = seq_len // tc
    tok_spec = pl.BlockSpec((tc, sub, LANES), lambda i: (i, 0, 0))
    return pl.pallas_call(
        _combine_kernel,
        grid=(n_tok // tc,),
        in_specs=[pl.BlockSpec((1, 1, tc * TOP_K), lambda i: (i, 0, 0), memory_space=pltpu.SMEM),
                  pl.BlockSpec(memory_space=pl.ANY),
                  tok_spec,
                  pl.BlockSpec((1, sub, LANES), lambda i: (i // tiles_per_seq, 0, 0)),
                  pl.BlockSpec((sub, LANES), lambda i: (0, 0))],
        out_specs=tok_spec,
        out_shape=jax.ShapeDtypeStruct(x3.shape, F32),
        scratch_shapes=[pltpu.VMEM((TOP_K, tc, sub, LANES), y3.dtype), pltpu.SemaphoreType.DMA(())],
        compiler_params=_params("arbitrary"),
        name="moe_combine",
    )(dest.reshape(n_tok // tc, 1, tc * TOP_K), y3, x3, gt3, pg3)


def _moe_ffn(x, mod, pre_g, router_w, router_b, w_gate, w_up, w_down, post_g):
    b, s, d = x.shape
    n_tok = b * s
    n_assign = n_tok * TOP_K
    sub = d // LANES
    tm = MOE_ROW_TILE
    h, gates, idx = _router(x, mod, pre_g, router_w, router_b)
    expert = idx[..., :TOP_K].reshape(n_assign)
    gate = gates[..., :TOP_K].reshape(n_assign)
    onehot = (expert[:, None] == jnp.arange(N_EXPERTS)[None, :]).astype(jnp.int32)
    csum = jnp.cumsum(onehot, axis=0)
    rank = jnp.sum((csum - onehot) * onehot, axis=1)
    counts = csum[-1]
    padded = (counts + tm - 1) // tm * tm
    pad_end = jnp.cumsum(padded)
    pad_start = pad_end - padded
    dest = (pad_start[expert] + rank).astype(jnp.int32)
    n_tiles = -(-n_assign // tm) + N_EXPERTS
    n_rows = n_tiles * tm
    n_valid = (pad_end[-1] // tm).astype(jnp.int32).reshape(1)
    tile_start = jnp.minimum(jnp.arange(n_tiles, dtype=jnp.int32), n_valid[0] - 1) * tm
    tile_expert = jnp.clip(jnp.searchsorted(pad_end, tile_start, side='right'), 0, N_EXPERTS - 1).astype(jnp.int32)
    gate_rows = jnp.zeros((n_rows, 1), F32).at[dest, 0].set(gate)

    rows3 = _dispatch(h.reshape(n_tok, sub, LANES), dest, n_rows)
    y = _experts(rows3.reshape(n_rows, d), gate_rows, tile_expert, n_valid,
                 w_gate.astype(BF16), w_up.astype(BF16), w_down.astype(BF16))
    gt3 = mod[:, 5].reshape(b, sub, LANES)
    out3 = _combine(y.reshape(n_rows, sub, LANES), dest, x.reshape(n_tok, sub, LANES), gt3,
                    post_g.reshape(sub, LANES), s)
    return out3.reshape(b, s, d)


def kernel(x, c, positions, ada_w, ada_b, mix_pre_g, mix_post_g, ffn_pre_g, ffn_post_g, w_in, w_out, group_out_g, gmlp_ws, gmlp_b, conv_w, conv_b, conv_ln_g, conv_ln_b, ffn_w_gate, ffn_w_up, ffn_w_down, router_w, router_b, moe_w_gate, moe_w_up, moe_w_down):
    depth = ada_w.shape[0]
    mod_all = _ada_mod(c, ada_w, ada_b)
    cos, sin = _rope_tables(positions)
    for layer in range(depth):
        mod = mod_all[layer]
        proj = _in_proj(x, mod, mix_pre_g[layer], w_in[layer].astype(BF16), cos, sin)
        groups = (_moba(proj), _dilated(proj),
                  _gmlp(proj, gmlp_ws[layer], gmlp_b[layer]),
                  _conv(proj, conv_w[layer], conv_b[layer], conv_ln_g[layer], conv_ln_b[layer]))
        x = _out_proj(groups, x, mod, group_out_g[layer], w_out[layer].astype(BF16), mix_post_g[layer])
        i = layer // 2
        if layer % 2 == 0:
            x = _dense_ffn(x, mod, ffn_pre_g[layer], ffn_w_gate[i].astype(BF16), ffn_w_up[i].astype(BF16),
                           ffn_w_down[i].astype(BF16), ffn_post_g[layer])
        else:
            x = _moe_ffn(x, mod, ffn_pre_g[layer], router_w[i], router_b[i],
                         moe_w_gate[i], moe_w_up[i], moe_w_down[i], ffn_post_g[layer])
    return x
```
